```python
import jax, jax.numpy as jnp
from jax import lax
import numpy as np

D_MODEL = 1024
BATCH = 16
SEQ = 2048
DEPTH = 2

MEM_LEN = 256
N_MIXERS = 2
N_RET_LAYERS = (DEPTH + 1) // 2
N_CONV_LAYERS = DEPTH // 2
MIX_WIDTH = 2 * D_MODEL
XATTN_HEADS = 4
XATTN_WIDTH = MIX_WIDTH // 4
XATTN_HEAD_DIM = XATTN_WIDTH // XATTN_HEADS
BRANCH_WIDTH = MIX_WIDTH - XATTN_WIDTH
RET_HEADS = 8
RET_QK_DIM = D_MODEL // RET_HEADS
RET_V_DIM = BRANCH_WIDTH // RET_HEADS
CHUNK = 128
CONV_WIDTH = 31
ROPE_BASE = 10000.0
EPS = 1e-6
RET_IN_WIDTH = 2 * D_MODEL + BRANCH_WIDTH + XATTN_WIDTH + MIX_WIDTH
CONV_IN_WIDTH = 2 * BRANCH_WIDTH + XATTN_WIDTH + MIX_WIDTH

kernel_name = "hybrid_retention_conformer_memxattn"


def rmsnorm(x, g):
    xf = x.astype(jnp.float32)
    y = xf * lax.rsqrt(jnp.mean(xf * xf, axis=-1, keepdims=True) + EPS)
    return (y * g.astype(jnp.float32)).astype(x.dtype)


def layernorm(x, g, b):
    xf = x.astype(jnp.float32)
    mu = jnp.mean(xf, axis=-1, keepdims=True)
    var = jnp.mean(jnp.square(xf - mu), axis=-1, keepdims=True)
    y = (xf - mu) * lax.rsqrt(var + EPS)
    return (y * g.astype(jnp.float32) + b.astype(jnp.float32)).astype(x.dtype)


def rotary(t, positions):
    half = t.shape[-1] // 2
    inv = ROPE_BASE ** (-jnp.arange(half, dtype=jnp.float32) / half)
    ang = positions.astype(jnp.float32)[..., None] * inv
    cos = jnp.cos(ang)[:, :, None, :]
    sin = jnp.sin(ang)[:, :, None, :]
    tf = t.astype(jnp.float32)
    t1, t2 = tf[..., :half], tf[..., half:]
    out = jnp.concatenate([t1 * cos - t2 * sin, t2 * cos + t1 * sin], axis=-1)
    return out.astype(t.dtype)


def chunkwise_retention(q, k, v):
    b, s, h, dk = q.shape
    dv = v.shape[-1]
    n_chunks = s // CHUNK
    log_gamma = jnp.log1p(-jnp.exp2(-5.0 - jnp.arange(h, dtype=jnp.float32)))
    idx = jnp.arange(CHUNK, dtype=jnp.float32)
    rel = idx[:, None] - idx[None, :]
    intra = jnp.where(rel[None] >= 0, jnp.exp(jnp.maximum(rel, 0.0)[None] * log_gamma[:, None, None]), 0.0)
    xi = jnp.exp((idx[:, None] + 1.0) * log_gamma[None, :])
    zeta = jnp.exp((CHUNK - 1.0 - idx[:, None]) * log_gamma[None, :])
    chunk_decay = jnp.exp(CHUNK * log_gamma)

    def to_chunks(t):
        return jnp.moveaxis(t.reshape(b, n_chunks, CHUNK, h, t.shape[-1]), 1, 0)

    def step(state, qkv):
        qn, kn, vn = qkv
        scores = jnp.einsum('bihd,bjhd->bhij', qn, kn) * intra[None]
        inner = jnp.einsum('bhij,bjhv->bihv', scores, vn)
        cross = jnp.einsum('bihd,bhdv->bihv', qn, state) * xi[None, :, :, None]
        new_state = state * chunk_decay[None, :, None, None] + jnp.einsum(
            'bjhd,bjhv->bhdv', kn * zeta[None, :, :, None], vn)
        return new_state, inner + cross

    state0 = jnp.zeros((b, h, dk, dv), jnp.float32)
    _, ys = lax.scan(step, state0, (to_chunks(q), to_chunks(k), to_chunks(v)))
    return jnp.moveaxis(ys, 0, 1).reshape(b, s, h, dv)


def retention_branch(proj, positions):
    b, s, _ = proj.shape
    q, k, v, xq, gate = jnp.split(
        proj, [D_MODEL, 2 * D_MODEL, 2 * D_MODEL + BRANCH_WIDTH,
               2 * D_MODEL + BRANCH_WIDTH + XATTN_WIDTH], axis=-1)
    q = rotary(q.reshape(b, s, RET_HEADS, RET_QK_DIM), positions)
    k = rotary(k.reshape(b, s, RET_HEADS, RET_QK_DIM), positions) * (RET_QK_DIM ** -0.5)
    v = v.reshape(b, s, RET_HEADS, RET_V_DIM)
    o = chunkwise_retention(q, k, v)
    mu = jnp.mean(o, axis=-1, keepdims=True)
    var = jnp.mean(jnp.square(o - mu), axis=-1, keepdims=True)
    o = ((o - mu) * lax.rsqrt(var + EPS)).reshape(b, s, BRANCH_WIDTH).astype(proj.dtype)
    return o, xq, gate


def conv_branch(proj, dw_w, dw_b, ln_g, ln_b):
    u, g, xq, gate = jnp.split(
        proj, [BRANCH_WIDTH, 2 * BRANCH_WIDTH, 2 * BRANCH_WIDTH + XATTN_WIDTH], axis=-1)
    glu = u * jax.nn.sigmoid(g)
    y = lax.conv_general_dilated(
        glu, dw_w[:, None, :].astype(glu.dtype), window_strides=(1,),
        padding=[(CONV_WIDTH - 1, 0)], dimension_numbers=('NWC', 'WIO', 'NWC'),
        feature_group_count=BRANCH_WIDTH) + dw_b.astype(glu.dtype)
    y = jax.nn.silu(layernorm(y, ln_g, ln_b))
    return y, xq, gate


def memory_attention(xq, mem_k, mem_v):
    b, s, _ = xq.shape
    q = xq.reshape(b, s, XATTN_HEADS, XATTN_HEAD_DIM)
    scores = jnp.einsum('bshd,bmhd->bhsm', q, mem_k).astype(jnp.float32) * (XATTN_HEAD_DIM ** -0.5)
    p = jax.nn.softmax(scores, axis=-1).astype(mem_v.dtype)
    o = jnp.einsum('bhsm,bmhd->bshd', p, mem_v)
    return o.reshape(b, s, XATTN_WIDTH)


def setup_inputs(seed: int = 0) -> dict:
    key = jax.random.key(seed)
    ks = jax.random.split(key, 20)
    f32 = jnp.float32
    nrm = lambda k, shape, scale: jax.random.normal(k, shape, f32) * scale
    x = jax.random.normal(ks[0], (BATCH, SEQ, D_MODEL), f32)
    mem = jax.random.normal(ks[1], (BATCH, MEM_LEN, D_MODEL), f32)
    offset = jax.random.randint(ks[2], (BATCH, 1), 0, 1024, dtype=jnp.int32)
    positions = offset + jnp.arange(SEQ, dtype=jnp.int32)[None, :]
    return {
        "x": x,
        "mem": mem,
        "positions": positions,
        "mem_norm_g": 1.0 + nrm(ks[3], (D_MODEL,), 0.02),
        "w_mem_kv": nrm(ks[4], (D_MODEL, 2 * XATTN_WIDTH), D_MODEL ** -0.5),
        "norm_pre_g": 1.0 + nrm(ks[5], (DEPTH, D_MODEL), 0.02),
        "norm_post_g": 1.0 + nrm(ks[6], (DEPTH, D_MODEL), 0.02),
        "ret_w_in": nrm(ks[7], (N_RET_LAYERS, D_MODEL, RET_IN_WIDTH), D_MODEL ** -0.5),
        "ret_w_out": nrm(ks[8], (N_RET_LAYERS, MIX_WIDTH, D_MODEL), MIX_WIDTH ** -0.5),
        "conv_w_in": nrm(ks[9], (N_CONV_LAYERS, D_MODEL, CONV_IN_WIDTH), D_MODEL ** -0.5),
        "conv_dw_w": nrm(ks[10], (N_CONV_LAYERS, CONV_WIDTH, BRANCH_WIDTH), CONV_WIDTH ** -0.5),
        "conv_dw_b": nrm(ks[11], (N_CONV_LAYERS, BRANCH_WIDTH), 0.02),
        "conv_ln_g": 1.0 + nrm(ks[12], (N_CONV_LAYERS, BRANCH_WIDTH), 0.02),
        "conv_ln_b": nrm(ks[13], (N_CONV_LAYERS, BRANCH_WIDTH), 0.02),
        "conv_w_out": nrm(ks[14], (N_CONV_LAYERS, MIX_WIDTH, D_MODEL), MIX_WIDTH ** -0.5),
    }


def reference(x, mem, positions, mem_norm_g, w_mem_kv, norm_pre_g, norm_post_g,
              ret_w_in, ret_w_out, conv_w_in, conv_dw_w, conv_dw_b, conv_ln_g,
              conv_ln_b, conv_w_out):
    b = x.shape[0]
    mem_kv = rmsnorm(mem, mem_norm_g) @ w_mem_kv
    mem_k, mem_v = jnp.split(mem_kv, 2, axis=-1)
    mem_k = mem_k.reshape(b, MEM_LEN, XATTN_HEADS, XATTN_HEAD_DIM)
    mem_v = mem_v.reshape(b, MEM_LEN, XATTN_HEADS, XATTN_HEAD_DIM)

    for i in range(DEPTH):
        h = rmsnorm(x, norm_pre_g[i])
        j = i // N_MIXERS
        if i % N_MIXERS == 0:
            branch, xq, gate = retention_branch(h @ ret_w_in[j], positions)
            w_out = ret_w_out[j]
        else:
            branch, xq, gate = conv_branch(h @ conv_w_in[j], conv_dw_w[j], conv_dw_b[j],
                                           conv_ln_g[j], conv_ln_b[j])
            w_out = conv_w_out[j]
        xa = memory_attention(xq, mem_k, mem_v)
        y = jnp.concatenate([branch, xa], axis=-1) * jax.nn.silu(gate)
        x = x + rmsnorm(y @ w_out, norm_post_g[i])
    return x
```

```python
import functools
import math

import jax
import jax.numpy as jnp
from jax import lax
from jax.experimental import pallas as pl
from jax.experimental.pallas import tpu as pltpu

D_MODEL = 1024
MEM_LEN = 256
MIX_WIDTH = 2 * D_MODEL
XATTN_HEADS = 4
XATTN_WIDTH = MIX_WIDTH // 4
XATTN_HEAD_DIM = XATTN_WIDTH // XATTN_HEADS
BRANCH_WIDTH = MIX_WIDTH - XATTN_WIDTH
RET_HEADS = 8
RET_QK_DIM = D_MODEL // RET_HEADS
RET_V_DIM = BRANCH_WIDTH // RET_HEADS
CONV_WIDTH = 31
ROPE_BASE = 10000.0
EPS = 1e-6

LANES = 128
SEQ_TILE = 256
HALO = 32
CONV_ROWS = 64
V_WINDOW = 2 * LANES
VMEM_LIMIT = 56 * 1024 * 1024

F32 = jnp.float32
BF16 = jnp.bfloat16

_LOG_GAMMA = [math.log1p(-(2.0 ** (-5.0 - h))) for h in range(RET_HEADS)]


def _dot(a, b):
    return jnp.dot(a, b, preferred_element_type=F32)


def _dot_nt(a, b):
    return lax.dot_general(a, b, (((1,), (1,)), ((), ())), preferred_element_type=F32)


def _dot_tn(a, b):
    return lax.dot_general(a, b, (((0,), (0,)), ((), ())), preferred_element_type=F32)


def _rmsnorm(xf, g):
    ms = jnp.mean(xf * xf, axis=-1, keepdims=True)
    return xf * lax.rsqrt(ms + EPS) * g


def _sigmoid(x):
    return 1.0 / (1.0 + jnp.exp(-x))


def _silu(x):
    return x * _sigmoid(x)


def _mem_attention(xq, mem_ref, head):
    lo = head * XATTN_HEAD_DIM
    qh = xq[:, lo:lo + XATTN_HEAD_DIM]
    kh = mem_ref[:, lo:lo + XATTN_HEAD_DIM]
    vh = mem_ref[:, XATTN_WIDTH + lo:XATTN_WIDTH + lo + XATTN_HEAD_DIM]
    s = _dot_nt(qh, kh) * (XATTN_HEAD_DIM ** -0.5)
    e = jnp.exp(s - jnp.max(s, axis=-1, keepdims=True))
    denom = jnp.sum(e, axis=-1, keepdims=True)
    return _dot(e.astype(BF16), vh) * (1.0 / denom)


def _mem_kv_kernel(mem_ref, g_ref, w_ref, o_ref):
    h = _rmsnorm(mem_ref[...], g_ref[...]).astype(BF16)
    o_ref[...] = _dot(h, w_ref[...]).astype(o_ref.dtype)


def _ret_layer_kernel(x_ref, pos_ref, inv_ref, mem_ref, gpre_ref, gpost_ref, win_ref, wout_ref,
                      o_ref, state_ref, y_ref):
    t = x_ref.shape[0]

    @pl.when(pl.program_id(1) == 0)
    def _():
        state_ref[...] = jnp.zeros_like(state_ref)

    x = x_ref[...]
    h = _rmsnorm(x, gpre_ref[...]).astype(BF16)

    lane = lax.broadcasted_iota(jnp.int32, (1, LANES), 1)
    ang = pos_ref[...].astype(F32) * inv_ref[...]
    cos2 = jnp.cos(ang)
    sin2 = jnp.sin(ang)
    sin_signed = jnp.where(lane < LANES // 2, -sin2, sin2)

    def rope(tq):
        return tq * cos2 + pltpu.roll(tq, LANES // 2, axis=1) * sin_signed

    q = _dot(h, win_ref[:, 0:D_MODEL])
    k = _dot(h, win_ref[:, D_MODEL:2 * D_MODEL])
    v = _dot(h, win_ref[:, 2 * D_MODEL:2 * D_MODEL + BRANCH_WIDTH]).astype(BF16)

    row = lax.broadcasted_iota(jnp.int32, (t, 1), 0).astype(F32)
    col = lax.broadcasted_iota(jnp.int32, (1, t), 1).astype(F32)
    rel = row - col
    causal = rel >= 0.0
    wlane = lax.broadcasted_iota(jnp.int32, (1, V_WINDOW), 1)

    gate_lo = 2 * D_MODEL + BRANCH_WIDTH + XATTN_WIDTH
    pair_w = 2 * RET_V_DIM
    for p in range(RET_HEADS // 2):
        normed = []
        for odd in range(2):
            hd = 2 * p + odd
            lg = _LOG_GAMMA[hd]
            qh = rope(q[:, hd * RET_QK_DIM:(hd + 1) * RET_QK_DIM])
            kh = rope(k[:, hd * RET_QK_DIM:(hd + 1) * RET_QK_DIM]) * (RET_QK_DIM ** -0.5)
            wlo = p * pair_w + odd * LANES
            vwin = v[:, wlo:wlo + V_WINDOW]
            valid = (wlane >= (LANES // 2)) if odd else (wlane < RET_V_DIM)
            decay = jnp.where(causal, jnp.exp(rel * lg), 0.0)
            xi = jnp.exp((row + 1.0) * lg)
            zeta = jnp.exp((t - 1.0 - row) * lg)
            qb = qh.astype(BF16)
            scores = _dot_nt(qb, kh.astype(BF16)) * decay
            inner = _dot(scores.astype(BF16), vwin)
            st = state_ref[hd]
            cross = _dot(qb, st.astype(BF16)) * xi
            state_ref[hd] = st * math.exp(t * lg) + _dot_tn((kh * zeta).astype(BF16), vwin)
            o = inner + cross
            mu = jnp.sum(jnp.where(valid, o, 0.0), axis=-1, keepdims=True) * (1.0 / RET_V_DIM)
            d = jnp.where(valid, o - mu, 0.0)
            var = jnp.sum(d * d, axis=-1, keepdims=True) * (1.0 / RET_V_DIM)
            normed.append(d * lax.rsqrt(var + EPS))
        ev, od = normed
        branch = jnp.concatenate(
            [ev[:, :LANES], ev[:, LANES:] + od[:, :LANES], od[:, LANES:]], axis=-1)
        clo = p * pair_w
        gate = _dot(h, win_ref[:, gate_lo + clo:gate_lo + clo + pair_w])
        y_ref[:, clo:clo + pair_w] = (branch * _silu(gate)).astype(BF16)

    xq_lo = 2 * D_MODEL + BRANCH_WIDTH
    xq = _dot(h, win_ref[:, xq_lo:xq_lo + XATTN_WIDTH]).astype(BF16)
    for hd in range(XATTN_HEADS):
        clo = BRANCH_WIDTH + hd * XATTN_HEAD_DIM
        gate = _dot(h, win_ref[:, gate_lo + clo:gate_lo + clo + XATTN_HEAD_DIM])
        y_ref[:, clo:clo + XATTN_HEAD_DIM] = (_mem_attention(xq, mem_ref, hd) * _silu(gate)).astype(BF16)

    r = _dot(y_ref[...], wout_ref[...])
    o_ref[...] = x + _rmsnorm(r, gpost_ref[...])


def _conv_layer_kernel(x_ref, mem_ref, gpre_ref, gpost_ref, win_ref, wout_ref, dww_ref, dwb_ref,
                       lng_ref, lnb_ref, o_ref, gbuf_ref, conv_ref, y_ref):
    t = x_ref.shape[0]

    @pl.when(pl.program_id(1) == 0)
    def _():
        gbuf_ref[0:HALO, :] = jnp.zeros((HALO, BRANCH_WIDTH), F32)

    x = x_ref[...]
    h = _rmsnorm(x, gpre_ref[...]).astype(BF16)

    u = _dot(h, win_ref[:, 0:BRANCH_WIDTH])
    g = _dot(h, win_ref[:, BRANCH_WIDTH:2 * BRANCH_WIDTH])
    gbuf_ref[HALO:HALO + t, :] = u * _sigmoid(g)

    shift = HALO - (CONV_WIDTH - 1)
    for cb in range(BRANCH_WIDTH // LANES):
        cs = slice(cb * LANES, (cb + 1) * LANES)
        for rb in range(t // CONV_ROWS):
            r0 = rb * CONV_ROWS
            acc = jnp.broadcast_to(dwb_ref[:, cs], (CONV_ROWS, LANES))
            for kk in range(CONV_WIDTH):
                acc = acc + gbuf_ref[r0 + shift + kk:r0 + shift + kk + CONV_ROWS, cs] * dww_ref[kk:kk + 1, cs]
            conv_ref[r0:r0 + CONV_ROWS, cs] = acc
    gbuf_ref[0:HALO, :] = gbuf_ref[t:t + HALO, :]

    c = conv_ref[...]
    mu = jnp.mean(c, axis=-1, keepdims=True)
    d = c - mu
    var = jnp.mean(d * d, axis=-1, keepdims=True)
    branch = _silu(d * lax.rsqrt(var + EPS) * lng_ref[...] + lnb_ref[...])

    gate_lo = 2 * BRANCH_WIDTH + XATTN_WIDTH
    gate = _dot(h, win_ref[:, gate_lo:gate_lo + BRANCH_WIDTH])
    y_ref[:, 0:BRANCH_WIDTH] = (branch * _silu(gate)).astype(BF16)

    xq_lo = 2 * BRANCH_WIDTH
    xq = _dot(h, win_ref[:, xq_lo:xq_lo + XATTN_WIDTH]).astype(BF16)
    for hd in range(XATTN_HEADS):
        clo = BRANCH_WIDTH + hd * XATTN_HEAD_DIM
        gate = _dot(h, win_ref[:, gate_lo + clo:gate_lo + clo + XATTN_HEAD_DIM])
        y_ref[:, clo:clo + XATTN_HEAD_DIM] = (_mem_attention(xq, mem_ref, hd) * _silu(gate)).astype(BF16)

    r = _dot(y_ref[...], wout_ref[...])
    o_ref[...] = x + _rmsnorm(r, gpost_ref[...])


def _resident(shape):
    return pl.BlockSpec(shape, lambda b, s: (0,) * len(shape), pipeline_mode=pl.Buffered(1))


def _layer_params():
    return pltpu.CompilerParams(dimension_semantics=("arbitrary", "arbitrary"),
                                vmem_limit_bytes=VMEM_LIMIT)


def _mem_kv(mem, g, w_bf):
    b = mem.shape[0]
    return pl.pallas_call(
        _mem_kv_kernel,
        out_shape=jax.ShapeDtypeStruct((b, MEM_LEN, 2 * XATTN_WIDTH), BF16),
        grid=(b,),
        in_specs=[
            pl.BlockSpec((None, MEM_LEN, D_MODEL), lambda i: (i, 0, 0)),
            pl.BlockSpec((1, D_MODEL), lambda i: (0, 0)),
            pl.BlockSpec((D_MODEL, 2 * XATTN_WIDTH), lambda i: (0, 0)),
        ],
        out_specs=pl.BlockSpec((None, MEM_LEN, 2 * XATTN_WIDTH), lambda i: (i, 0, 0)),
        compiler_params=pltpu.CompilerParams(dimension_semantics=("arbitrary",)),
        name="mem_kv",
    )(mem, g, w_bf)


def _ret_layer(x, pos3, inv2, mem_kv, gpre, gpost, win_bf, wout_bf):
    b, s, d = x.shape
    t = SEQ_TILE
    tile = lambda w: pl.BlockSpec((None, t, w), lambda i, j: (i, j, 0))
    return pl.pallas_call(
        _ret_layer_kernel,
        out_shape=jax.ShapeDtypeStruct(x.shape, x.dtype),
        grid=(b, s // t),
        in_specs=[
            tile(d),
            tile(1),
            _resident((1, LANES)),
            pl.BlockSpec((None, MEM_LEN, 2 * XATTN_WIDTH), lambda i, j: (i, 0, 0)),
            _resident((1, d)),
            _resident((1, d)),
            _resident(win_bf.shape),
            _resident(wout_bf.shape),
        ],
        out_specs=tile(d),
        scratch_shapes=[
            pltpu.VMEM((RET_HEADS, RET_QK_DIM, V_WINDOW), F32),
            pltpu.VMEM((t, MIX_WIDTH), BF16),
        ],
        compiler_params=_layer_params(),
        name="ret_layer",
    )(x, pos3, inv2, mem_kv, gpre, gpost, win_bf, wout_bf)


def _conv_layer(x, mem_kv, gpre, gpost, win_bf, wout_bf, dww, dwb, lng, lnb):
    b, s, d = x.shape
    t = SEQ_TILE
    tile = lambda w: pl.BlockSpec((None, t, w), lambda i, j: (i, j, 0))
    return pl.pallas_call(
        _conv_layer_kernel,
        out_shape=jax.ShapeDtypeStruct(x.shape, x.dtype),
        grid=(b, s // t),
        in_specs=[
            tile(d),
            pl.BlockSpec((None, MEM_LEN, 2 * XATTN_WIDTH), lambda i, j: (i, 0, 0)),
            _resident((1, d)),
            _resident((1, d)),
            _resident(win_bf.shape),
            _resident(wout_bf.shape),
            _resident(dww.shape),
            _resident((1, BRANCH_WIDTH)),
            _resident((1, BRANCH_WIDTH)),
            _resident((1, BRANCH_WIDTH)),
        ],
        out_specs=tile(d),
        scratch_shapes=[
            pltpu.VMEM((t + HALO, BRANCH_WIDTH), F32),
            pltpu.VMEM((t, BRANCH_WIDTH), F32),
            pltpu.VMEM((t, MIX_WIDTH), BF16),
        ],
        compiler_params=_layer_params(),
        name="conv_layer",
    )(x, mem_kv, gpre, gpost, win_bf, wout_bf, dww, dwb, lng, lnb)


def kernel(x, mem, positions, mem_norm_g, w_mem_kv, norm_pre_g, norm_post_g, ret_w_in, ret_w_out,
           conv_w_in, conv_dw_w, conv_dw_b, conv_ln_g, conv_ln_b, conv_w_out):
    depth = norm_pre_g.shape[0]
    half = RET_QK_DIM // 2
    inv = ROPE_BASE ** (-jnp.arange(half, dtype=F32) / half)
    inv2 = jnp.concatenate([inv, inv])[None, :]
    pos3 = positions[:, :, None]
    row = lambda a: a[None, :]

    mem_kv = _mem_kv(mem, row(mem_norm_g), w_mem_kv.astype(BF16))
    for i in range(depth):
        j = i // 2
        if i % 2 == 0:
            x = _ret_layer(x, pos3, inv2, mem_kv, row(norm_pre_g[i]), row(norm_post_g[i]),
                           ret_w_in[j].astype(BF16), ret_w_out[j].astype(BF16))
        else:
            x = _conv_layer(x, mem_kv, row(norm_pre_g[i]), row(norm_post_g[i]),
                            conv_w_in[j].astype(BF16), conv_w_out[j].astype(BF16),
                            conv_dw_w[j], row(conv_dw_b[j]), row(conv_ln_g[j]), row(conv_ln_b[j]))
    return x
```

```python
import functools
import math

import jax
import jax.numpy as jnp
from jax import lax
from jax.experimental import pallas as pl
from jax.experimental.pallas import tpu as pltpu

D_MODEL = 1024
MEM_LEN = 256
MIX_WIDTH = 2 * D_MODEL
XATTN_HEADS = 4
XATTN_WIDTH = MIX_WIDTH // 4
XATTN_HEAD_DIM = XATTN_WIDTH // XATTN_HEADS
BRANCH_WIDTH = MIX_WIDTH - XATTN_WIDTH
RET_HEADS = 8
RET_QK_DIM = D_MODEL // RET_HEADS
RET_V_DIM = BRANCH_WIDTH // RET_HEADS
CONV_WIDTH = 31
ROPE_BASE = 10000.0
EPS = 1e-6

LANES = 128
SEQ_TILE = 256
HALO = 32
CONV_ROWS = 64
V_WINDOW = 2 * LANES
VMEM_LIMIT = 56 * 1024 * 1024

F32 = jnp.float32
BF16 = jnp.bfloat16

_LOG_GAMMA = [math.log1p(-(2.0 ** (-5.0 - h))) for h in range(RET_HEADS)]


def _dot(a, b):
    return jnp.dot(a, b, preferred_element_type=F32)


def _dot_nt(a, b):
    return lax.dot_general(a, b, (((1,), (1,)), ((), ())), preferred_element_type=F32)


def _dot_tn(a, b):
    return lax.dot_general(a, b, (((0,), (0,)), ((), ())), preferred_element_type=F32)


def _rmsnorm(xf, g):
    ms = jnp.mean(xf * xf, axis=-1, keepdims=True)
    return xf * lax.rsqrt(ms + EPS) * g


def _sigmoid(x):
    return 1.0 / (1.0 + jnp.exp(-x))


def _silu(x):
    return x * _sigmoid(x)


def _mem_attention(xq, mem_ref, head):
    lo = head * XATTN_HEAD_DIM
    qh = xq[:, lo:lo + XATTN_HEAD_DIM]
    kh = mem_ref[:, lo:lo + XATTN_HEAD_DIM]
    vh = mem_ref[:, XATTN_WIDTH + lo:XATTN_WIDTH + lo + XATTN_HEAD_DIM]
    s = _dot_nt(qh, kh) * (XATTN_HEAD_DIM ** -0.5)
    e = jnp.exp(s - jnp.max(s, axis=-1, keepdims=True))
    denom = jnp.sum(e, axis=-1, keepdims=True)
    return _dot(e.astype(BF16), vh) * (1.0 / denom)


def _mem_kv_kernel(mem_ref, g_ref, w_ref, o_ref):
    h = _rmsnorm(mem_ref[...], g_ref[...]).astype(BF16)
    o_ref[...] = _dot(h, w_ref[...]).astype(o_ref.dtype)


def _ret_layer_kernel(x_ref, pos_ref, inv_ref, mem_ref, gpre_ref, gpost_ref, win_ref, wout_ref,
                      o_ref, state_ref, y_ref):
    t = x_ref.shape[0]

    @pl.when(pl.program_id(1) == 0)
    def _():
        state_ref[...] = jnp.zeros_like(state_ref)

    x = x_ref[...]
    h = _rmsnorm(x, gpre_ref[...]).astype(BF16)

    lane = lax.broadcasted_iota(jnp.int32, (1, LANES), 1)
    ang = pos_ref[...].astype(F32) * inv_ref[...]
    cos2 = jnp.cos(ang)
    sin2 = jnp.sin(ang)
    sin_signed = jnp.where(lane < LANES // 2, -sin2, sin2)

    def rope(tq):
        return tq * cos2 + pltpu.roll(tq, LANES // 2, axis=1) * sin_signed

    q = _dot(h, win_ref[:, 0:D_MODEL])
    k = _dot(h, win_ref[:, D_MODEL:2 * D_MODEL])
    v = _dot(h, win_ref[:, 2 * D_MODEL:2 * D_MODEL + BRANCH_WIDTH]).astype(BF16)

    row = lax.broadcasted_iota(jnp.int32, (t, 1), 0).astype(F32)
    col = lax.broadcasted_iota(jnp.int32, (1, t), 1).astype(F32)
    rel = row - col
    causal = rel >= 0.0
    wlane = lax.broadcasted_iota(jnp.int32, (1, V_WINDOW), 1)

    gate_lo = 2 * D_MODEL + BRANCH_WIDTH + XATTN_WIDTH
    pair_w = 2 * RET_V_DIM
    for p in range(RET_HEADS // 2):
        normed = []
        for odd in range(2):
            hd = 2 * p + odd
            lg = _LOG_GAMMA[hd]
            qh = rope(q[:, hd * RET_QK_DIM:(hd + 1) * RET_QK_DIM])
            kh = rope(k[:, hd * RET_QK_DIM:(hd + 1) * RET_QK_DIM]) * (RET_QK_DIM ** -0.5)
            wlo = p * pair_w + odd * LANES
            vwin = v[:, wlo:wlo + V_WINDOW]
            valid = (wlane >= (LANES // 2)) if odd else (wlane < RET_V_DIM)
            decay = jnp.where(causal, jnp.exp(rel * lg), 0.0)
            xi = jnp.exp((row + 1.0) * lg)
            zeta = jnp.exp((t - 1.0 - row) * lg)
            qb = qh.astype(BF16)
            scores = _dot_nt(qb, kh.astype(BF16)) * decay
            inner = _dot(scores.astype(BF16), vwin)
            st = state_ref[hd]
            cross = _dot(qb, st.astype(BF16)) * xi
            state_ref[hd] = st * math.exp(t * lg) + _dot_tn((kh * zeta).astype(BF16), vwin)
            o = inner + cross
            mu = jnp.sum(jnp.where(valid, o, 0.0), axis=-1, keepdims=True) * (1.0 / RET_V_DIM)
            d = jnp.where(valid, o - mu, 0.0)
            var = jnp.sum(d * d, axis=-1, keepdims=True) * (1.0 / RET_V_DIM)
            normed.append(d * lax.rsqrt(var + EPS))
        ev, od = normed
        branch = jnp.concatenate(
            [ev[:, :LANES], ev[:, LANES:] + od[:, :LANES], od[:, LANES:]], axis=-1)
        clo = p * pair_w
        gate = _dot(h, win_ref[:, gate_lo + clo:gate_lo + clo + pair_w])
        y_ref[:, clo:clo + pair_w] = (branch * _silu(gate)).astype(BF16)

    xq_lo = 2 * D_MODEL + BRANCH_WIDTH
    xq = _dot(h, win_ref[:, xq_lo:xq_lo + XATTN_WIDTH]).astype(BF16)
    for hd in range(XATTN_HEADS):
        clo = BRANCH_WIDTH + hd * XATTN_HEAD_DIM
        gate = _dot(h, win_ref[:, gate_lo + clo:gate_lo + clo + XATTN_HEAD_DIM])
        y_ref[:, clo:clo + XATTN_HEAD_DIM] = (_mem_attention(xq, mem_ref, hd) * _silu(gate)).astype(BF16)

    r = _dot(y_ref[...], wout_ref[...])
    o_ref[...] = x + _rmsnorm(r, gpost_ref[...])


def _conv_layer_kernel(x_ref, mem_ref, gpre_ref, gpost_ref, win_ref, wout_ref, dww_ref, dwb_ref,
                       lng_ref, lnb_ref, o_ref, gbuf_ref, conv_ref, y_ref):
    t = x_ref.shape[0]

    n_cb = BRANCH_WIDTH // LANES

    @pl.when(pl.program_id(1) == 0)
    def _():
        gbuf_ref[:, 0:HALO, :] = jnp.zeros((n_cb, HALO, LANES), F32)

    x = x_ref[...]
    h = _rmsnorm(x, gpre_ref[...]).astype(BF16)

    u = _dot(h, win_ref[:, 0:BRANCH_WIDTH])
    g = _dot(h, win_ref[:, BRANCH_WIDTH:2 * BRANCH_WIDTH])
    glu = u * _sigmoid(g)
    for cb in range(n_cb):
        gbuf_ref[cb, HALO:HALO + t, :] = glu[:, cb * LANES:(cb + 1) * LANES]

    shift = HALO - (CONV_WIDTH - 1)
    for cb in range(n_cb):
        cs = slice(cb * LANES, (cb + 1) * LANES)
        for rb in range(t // CONV_ROWS):
            r0 = rb * CONV_ROWS
            acc = jnp.broadcast_to(dwb_ref[:, cs], (CONV_ROWS, LANES))
            for kk in range(CONV_WIDTH):
                win = gbuf_ref[cb, pl.ds(r0 + shift + kk, CONV_ROWS, stride=1), :]
                acc = acc + win * dww_ref[kk:kk + 1, cs]
            conv_ref[r0:r0 + CONV_ROWS, cs] = acc
        gbuf_ref[cb, 0:HALO, :] = gbuf_ref[cb, t:t + HALO, :]

    c = conv_ref[...]
    mu = jnp.mean(c, axis=-1, keepdims=True)
    d = c - mu
    var = jnp.mean(d * d, axis=-1, keepdims=True)
    branch = _silu(d * lax.rsqrt(var + EPS) * lng_ref[...] + lnb_ref[...])

    gate_lo = 2 * BRANCH_WIDTH + XATTN_WIDTH
    gate = _dot(h, win_ref[:, gate_lo:gate_lo + BRANCH_WIDTH])
    y_ref[:, 0:BRANCH_WIDTH] = (branch * _silu(gate)).astype(BF16)

    xq_lo = 2 * BRANCH_WIDTH
    xq = _dot(h, win_ref[:, xq_lo:xq_lo + XATTN_WIDTH]).astype(BF16)
    for hd in range(XATTN_HEADS):
        clo = BRANCH_WIDTH + hd * XATTN_HEAD_DIM
        gate = _dot(h, win_ref[:, gate_lo + clo:gate_lo + clo + XATTN_HEAD_DIM])
        y_ref[:, clo:clo + XATTN_HEAD_DIM] = (_mem_attention(xq, mem_ref, hd) * _silu(gate)).astype(BF16)

    r = _dot(y_ref[...], wout_ref[...])
    o_ref[...] = x + _rmsnorm(r, gpost_ref[...])


def _resident(shape):
    return pl.BlockSpec(shape, lambda b, s: (0,) * len(shape), pipeline_mode=pl.Buffered(1))


def _layer_params():
    return pltpu.CompilerParams(dimension_semantics=("arbitrary", "arbitrary"),
                                vmem_limit_bytes=VMEM_LIMIT)


def _mem_kv(mem, g, w_bf):
    b = mem.shape[0]
    return pl.pallas_call(
        _mem_kv_kernel,
        out_shape=jax.ShapeDtypeStruct((b, MEM_LEN, 2 * XATTN_WIDTH), BF16),
        grid=(b,),
        in_specs=[
            pl.BlockSpec((None, MEM_LEN, D_MODEL), lambda i: (i, 0, 0)),
            pl.BlockSpec((1, D_MODEL), lambda i: (0, 0)),
            pl.BlockSpec((D_MODEL, 2 * XATTN_WIDTH), lambda i: (0, 0)),
        ],
        out_specs=pl.BlockSpec((None, MEM_LEN, 2 * XATTN_WIDTH), lambda i: (i, 0, 0)),
        compiler_params=pltpu.CompilerParams(dimension_semantics=("arbitrary",)),
        name="mem_kv",
    )(mem, g, w_bf)


def _ret_layer(x, pos3, inv2, mem_kv, gpre, gpost, win_bf, wout_bf):
    b, s, d = x.shape
    t = SEQ_TILE
    tile = lambda w: pl.BlockSpec((None, t, w), lambda i, j: (i, j, 0))
    return pl.pallas_call(
        _ret_layer_kernel,
        out_shape=jax.ShapeDtypeStruct(x.shape, x.dtype),
        grid=(b, s // t),
        in_specs=[
            tile(d),
            tile(1),
            _resident((1, LANES)),
            pl.BlockSpec((None, MEM_LEN, 2 * XATTN_WIDTH), lambda i, j: (i, 0, 0)),
            _resident((1, d)),
            _resident((1, d)),
            _resident(win_bf.shape),
            _resident(wout_bf.shape),
        ],
        out_specs=tile(d),
        scratch_shapes=[
            pltpu.VMEM((RET_HEADS, RET_QK_DIM, V_WINDOW), F32),
            pltpu.VMEM((t, MIX_WIDTH), BF16),
        ],
        compiler_params=_layer_params(),
        name="ret_layer",
    )(x, pos3, inv2, mem_kv, gpre, gpost, win_bf, wout_bf)


def _conv_layer(x, mem_kv, gpre, gpost, win_bf, wout_bf, dww, dwb, lng, lnb):
    b, s, d = x.shape
    t = SEQ_TILE
    tile = lambda w: pl.BlockSpec((None, t, w), lambda i, j: (i, j, 0))
    return pl.pallas_call(
        _conv_layer_kernel,
        out_shape=jax.ShapeDtypeStruct(x.shape, x.dtype),
        grid=(b, s // t),
        in_specs=[
            tile(d),
            pl.BlockSpec((None, MEM_LEN, 2 * XATTN_WIDTH), lambda i, j: (i, 0, 0)),
            _resident((1, d)),
            _resident((1, d)),
            _resident(win_bf.shape),
            _resident(wout_bf.shape),
            _resident(dww.shape),
            _resident((1, BRANCH_WIDTH)),
            _resident((1, BRANCH_WIDTH)),
            _resident((1, BRANCH_WIDTH)),
        ],
        out_specs=tile(d),
        scratch_shapes=[
            pltpu.VMEM((BRANCH_WIDTH // LANES, t + HALO, LANES), F32),
            pltpu.VMEM((t, BRANCH_WIDTH), F32),
            pltpu.VMEM((t, MIX_WIDTH), BF16),
        ],
        compiler_params=_layer_params(),
        name="conv_layer",
    )(x, mem_kv, gpre, gpost, win_bf, wout_bf, dww, dwb, lng, lnb)


def kernel(x, mem, positions, mem_norm_g, w_mem_kv, norm_pre_g, norm_post_g, ret_w_in, ret_w_out,
           conv_w_in, conv_dw_w, conv_dw_b, conv_ln_g, conv_ln_b, conv_w_out):
    depth = norm_pre_g.shape[0]
    half = RET_QK_DIM // 2
    inv = ROPE_BASE ** (-jnp.arange(half, dtype=F32) / half)
    inv2 = jnp.concatenate([inv, inv])[None, :]
    pos3 = positions[:, :, None]
    row = lambda a: a[None, :]

    mem_kv = _mem_kv(mem, row(mem_norm_g), w_mem_kv.astype(BF16))
    for i in range(depth):
        j = i // 2
        if i % 2 == 0:
            x = _ret_layer(x, pos3, inv2, mem_kv, row(norm_pre_g[i]), row(norm_post_g[i]),
                           ret_w_in[j].astype(BF16), ret_w_out[j].astype(BF16))
        else:
            x = _conv_layer(x, mem_kv, row(norm_pre_g[i]), row(norm_post_g[i]),
                            conv_w_in[j].astype(BF16), conv_w_out[j].astype(BF16),
                            conv_dw_w[j], row(conv_dw_b[j]), row(conv_ln_g[j]), row(conv_ln_b[j]))
    return x
```
